```python
import jax
import jax.numpy as jnp
from jax import lax
import numpy as np

D_MODEL = 1024
BATCH = 2
SEQ = 8192
DEPTH = 2

HEAD_DIM = 64
SWA_Q_HEADS = 8
SWA_KV_HEADS = 2
SWA_WINDOW = 128
SWA_BLOCK = 128
MOBA_HEADS = 8
MOBA_BLOCK = 256
MOBA_TOPK = 3
MOBA_Q_CHUNK = 128
GLA_HEADS = 4
GLA_DK = 64
GLA_DV = 128
GLA_GATE_RANK = 16
GLA_GATE_TEMP = 16.0
GLA_CHUNK = 64
N_BRANCH = 3
SWA_OUT_W = SWA_Q_HEADS * HEAD_DIM
MOBA_OUT_W = MOBA_HEADS * HEAD_DIM
GLA_OUT_W = GLA_HEADS * GLA_DV
D_FF = 2816
N_EXPERTS = 8
TOP_K = 2
D_FF_EXPERT = 1408
RMS_EPS = 1e-6
N_DENSE = (DEPTH + 1) // 2
N_MOE = DEPTH // 2
IN_WIDTHS = (SWA_Q_HEADS * HEAD_DIM, SWA_KV_HEADS * HEAD_DIM, SWA_KV_HEADS * HEAD_DIM,
             MOBA_HEADS * HEAD_DIM, MOBA_HEADS * HEAD_DIM, MOBA_HEADS * HEAD_DIM,
             GLA_HEADS * GLA_DK, GLA_HEADS * GLA_DK, GLA_HEADS * GLA_DV, GLA_HEADS * GLA_DV,
             GLA_GATE_RANK, N_BRANCH * D_MODEL)
IN_TOTAL = sum(IN_WIDTHS)

kernel_name = 'hybrid_swa_moba_gla_moe_block'


def rms_norm(x, g):
    xf = x.astype(jnp.float32)
    y = xf * lax.rsqrt(jnp.mean(xf * xf, axis=-1, keepdims=True) + RMS_EPS)
    return (y * g.astype(jnp.float32)).astype(x.dtype)


def swa_sink_attention(q, k, v, sinks):
    B, S, _, dh = q.shape
    nb = S // SWA_BLOCK
    G = SWA_Q_HEADS // SWA_KV_HEADS
    qb = q.reshape(B, nb, SWA_BLOCK, SWA_KV_HEADS, G, dh)
    kb = k.reshape(B, nb, SWA_BLOCK, SWA_KV_HEADS, dh)
    vb = v.reshape(B, nb, SWA_BLOCK, SWA_KV_HEADS, dh)
    zk = jnp.zeros_like(kb[:, :1])
    k2 = jnp.concatenate([jnp.concatenate([zk, kb[:, :-1]], axis=1), kb], axis=2)
    v2 = jnp.concatenate([jnp.concatenate([zk, vb[:, :-1]], axis=1), vb], axis=2)
    s = jnp.einsum('bnqhgd,bnkhd->bnhgqk', qb, k2).astype(jnp.float32) * (dh ** -0.5)
    qi = jnp.arange(SWA_BLOCK)[:, None]
    kj = jnp.arange(2 * SWA_BLOCK)[None, :]
    rel = qi + SWA_BLOCK - kj
    band = (rel >= 0) & (rel < SWA_WINDOW)
    blk = jnp.arange(nb)[:, None, None]
    mask = band[None] & ((blk > 0) | (kj >= SWA_BLOCK)[None])
    s = jnp.where(mask[None, :, None, None], s, -jnp.inf)
    sink = sinks.astype(jnp.float32).reshape(SWA_KV_HEADS, G)[None, None, :, :, None, None]
    sink = jnp.broadcast_to(sink, s.shape[:-1] + (1,))
    p = jax.nn.softmax(jnp.concatenate([s, sink], axis=-1), axis=-1)[..., :-1]
    o = jnp.einsum('bnhgqk,bnkhd->bnqhgd', p.astype(v.dtype), v2)
    return o.reshape(B, S, SWA_Q_HEADS * dh)


def moba_attention(q, k, v):
    B, S, H, dh = q.shape
    nblk = -(-S // MOBA_BLOCK)
    s_pad = nblk * MOBA_BLOCK
    padw = ((0, 0), (0, s_pad - S), (0, 0), (0, 0))
    kb = jnp.pad(k, padw).reshape(B, nblk, MOBA_BLOCK, H, dh).transpose(0, 3, 1, 2, 4)
    vb = jnp.pad(v, padw).reshape(B, nblk, MOBA_BLOCK, H, dh).transpose(0, 3, 1, 2, 4)
    kbar = jnp.mean(kb.astype(jnp.float32), axis=3).astype(q.dtype)
    qh = q.transpose(0, 2, 1, 3)
    topk = min(MOBA_TOPK, nblk)
    nq = S // MOBA_Q_CHUNK
    scale = dh ** -0.5
    bidx = jnp.arange(B)[:, None, None, None]
    hidx = jnp.arange(H)[None, :, None, None]
    blk_ids = jnp.arange(nblk)

    def chunk(ci):
        start = ci * MOBA_Q_CHUNK
        qc = lax.dynamic_slice_in_dim(qh, start, MOBA_Q_CHUNK, axis=2)
        cur = start // MOBA_BLOCK
        qpos = start + jnp.arange(MOBA_Q_CHUNK)
        gs = jnp.einsum('bhqd,bhnd->bhqn', qc, kbar).astype(jnp.float32)
        gs = jnp.where(blk_ids < cur, gs, -jnp.inf)
        _, idx = lax.top_k(gs, topk)
        valid = jnp.arange(topk) < cur
        kg = kb[bidx, hidx, idx]
        vg = vb[bidx, hidx, idx]
        s_sel = jnp.einsum('bhqd,bhqrkd->bhqrk', qc, kg).astype(jnp.float32) * scale
        s_sel = jnp.where(valid[:, None], s_sel, -jnp.inf).reshape(B, H, MOBA_Q_CHUNK, topk * MOBA_BLOCK)
        ko = lax.dynamic_index_in_dim(kb, cur, axis=2, keepdims=False)
        vo = lax.dynamic_index_in_dim(vb, cur, axis=2, keepdims=False)
        kpos = cur * MOBA_BLOCK + jnp.arange(MOBA_BLOCK)
        s_own = jnp.einsum('bhqd,bhkd->bhqk', qc, ko).astype(jnp.float32) * scale
        s_own = jnp.where(kpos[None, :] <= qpos[:, None], s_own, -jnp.inf)
        p = jax.nn.softmax(jnp.concatenate([s_sel, s_own], axis=-1), axis=-1).astype(v.dtype)
        p_sel = p[..., :topk * MOBA_BLOCK].reshape(B, H, MOBA_Q_CHUNK, topk, MOBA_BLOCK)
        p_own = p[..., topk * MOBA_BLOCK:]
        return (jnp.einsum('bhqrk,bhqrkd->bhqd', p_sel, vg)
                + jnp.einsum('bhqk,bhkd->bhqd', p_own, vo))

    out = lax.map(chunk, jnp.arange(nq))
    return out.transpose(1, 0, 3, 2, 4).reshape(B, S, H * dh)


def gla_attention(q, k, v, g_log):
    B, S, H, dk = q.shape
    dv = v.shape[-1]
    C = GLA_CHUNK
    nc = S // C

    def to_chunks(t):
        return t.astype(jnp.float32).reshape(B, nc, C, H, t.shape[-1]).transpose(0, 3, 1, 2, 4)

    qc = to_chunks(q) * (dk ** -0.5)
    kc = to_chunks(k)
    vc = to_chunks(v)
    b = jnp.cumsum(to_chunks(g_log), axis=3)
    b_last = b[:, :, :, -1:, :]
    q_in = qc * jnp.exp(b)
    k_in = kc * jnp.exp(-b)
    causal = jnp.tril(jnp.ones((C, C), dtype=bool))
    a = jnp.where(causal, jnp.einsum('bhntd,bhnsd->bhnts', q_in, k_in), 0.0)
    o_intra = jnp.einsum('bhnts,bhnse->bhnte', a, vc)
    u = jnp.einsum('bhnsd,bhnse->bhnde', kc * jnp.exp(b_last - b), vc)
    decay = jnp.exp(b_last[:, :, :, 0, :])

    def step(state, xs):
        u_n, d_n = xs
        return d_n[..., None] * state + u_n, state

    _, s_prev = lax.scan(step, jnp.zeros((B, H, dk, dv), jnp.float32),
                         (u.transpose(2, 0, 1, 3, 4), decay.transpose(2, 0, 1, 3)))
    s_prev = s_prev.transpose(1, 2, 0, 3, 4)
    o = o_intra + jnp.einsum('bhntd,bhnde->bhnte', q_in, s_prev)
    return o.transpose(0, 2, 3, 1, 4).reshape(B, S, H, dv)


def mixer_block(h, w_in, sinks, gla_w_gate, gla_b_gate, gla_norm_g, w_branch_a, w_branch_b, w_branch_c, w_o):
    B, S, _ = h.shape
    split_idx = [int(c) for c in np.cumsum(IN_WIDTHS)[:-1]]
    (a_q, a_k, a_v, b_q, b_k, b_v, c_q, c_k, c_v, c_r, c_g, gates) = jnp.split(h @ w_in, split_idx, axis=-1)
    o_a = swa_sink_attention(a_q.reshape(B, S, SWA_Q_HEADS, HEAD_DIM),
                             a_k.reshape(B, S, SWA_KV_HEADS, HEAD_DIM),
                             a_v.reshape(B, S, SWA_KV_HEADS, HEAD_DIM), sinks)
    o_b = moba_attention(b_q.reshape(B, S, MOBA_HEADS, HEAD_DIM),
                         b_k.reshape(B, S, MOBA_HEADS, HEAD_DIM),
                         b_v.reshape(B, S, MOBA_HEADS, HEAD_DIM))
    g_log = jax.nn.log_sigmoid((c_g @ gla_w_gate + gla_b_gate).astype(jnp.float32)) / GLA_GATE_TEMP
    o_c = gla_attention(c_q.reshape(B, S, GLA_HEADS, GLA_DK), c_k.reshape(B, S, GLA_HEADS, GLA_DK),
                        c_v.reshape(B, S, GLA_HEADS, GLA_DV), g_log.reshape(B, S, GLA_HEADS, GLA_DK))
    o_c = o_c * lax.rsqrt(jnp.mean(o_c * o_c, axis=-1, keepdims=True) + RMS_EPS)
    o_c = o_c * gla_norm_g.astype(jnp.float32).reshape(GLA_HEADS, GLA_DV)
    o_c = (o_c.reshape(B, S, GLA_OUT_W) * jax.nn.silu(c_r.astype(jnp.float32))).astype(h.dtype)
    g = jax.nn.sigmoid(gates).reshape(B, S, N_BRANCH, D_MODEL)
    y = g[:, :, 0] * (o_a @ w_branch_a) + g[:, :, 1] * (o_b @ w_branch_b) + g[:, :, 2] * (o_c @ w_branch_c)
    return y @ w_o


def swiglu(h, w_gate, w_up, w_down):
    return (jax.nn.silu(h @ w_gate) * (h @ w_up)) @ w_down


def moe_swiglu(h, w_router, w_gate, w_up, w_down):
    B, S, D = h.shape
    t = h.reshape(B * S, D)
    logits = (t @ w_router).astype(jnp.float32)
    top_v, top_i = lax.top_k(logits, TOP_K)
    top_p = jax.nn.softmax(top_v, axis=-1)
    combine = jnp.sum(jax.nn.one_hot(top_i, N_EXPERTS, dtype=jnp.float32) * top_p[..., None], axis=1)
    out = jnp.zeros_like(t)
    for e in range(N_EXPERTS):
        out = out + combine[:, e:e + 1].astype(t.dtype) * swiglu(t, w_gate[e], w_up[e], w_down[e])
    return out.reshape(B, S, D)


def setup_inputs(seed: int = 0) -> dict:
    key = jax.random.key(seed)
    ks = jax.random.split(key, 20)
    f32 = jnp.float32

    def nrm(k, shape, scale):
        return jax.random.normal(k, shape, f32) * scale

    def gain(k, shape):
        return 1.0 + 0.02 * jax.random.normal(k, shape, f32)

    return {
        'x': nrm(ks[0], (BATCH, SEQ, D_MODEL), 1.0),
        'norm1_g': gain(ks[1], (DEPTH, D_MODEL)),
        'w_in': nrm(ks[2], (DEPTH, D_MODEL, IN_TOTAL), D_MODEL ** -0.5),
        'swa_sinks': nrm(ks[3], (DEPTH, SWA_Q_HEADS), 0.5),
        'gla_w_gate': nrm(ks[4], (DEPTH, GLA_GATE_RANK, GLA_HEADS * GLA_DK), GLA_GATE_RANK ** -0.5),
        'gla_b_gate': nrm(ks[5], (DEPTH, GLA_HEADS * GLA_DK), 0.1),
        'gla_norm_g': gain(ks[6], (DEPTH, GLA_OUT_W)),
        'w_branch_a': nrm(ks[7], (DEPTH, SWA_OUT_W, D_MODEL), SWA_OUT_W ** -0.5),
        'w_branch_b': nrm(ks[8], (DEPTH, MOBA_OUT_W, D_MODEL), MOBA_OUT_W ** -0.5),
        'w_branch_c': nrm(ks[9], (DEPTH, GLA_OUT_W, D_MODEL), GLA_OUT_W ** -0.5),
        'w_o': nrm(ks[10], (DEPTH, D_MODEL, D_MODEL), D_MODEL ** -0.5),
        'norm2_g': gain(ks[11], (DEPTH, D_MODEL)),
        'ffn_w_gate': nrm(ks[12], (N_DENSE, D_MODEL, D_FF), D_MODEL ** -0.5),
        'ffn_w_up': nrm(ks[13], (N_DENSE, D_MODEL, D_FF), D_MODEL ** -0.5),
        'ffn_w_down': nrm(ks[14], (N_DENSE, D_FF, D_MODEL), D_FF ** -0.5),
        'moe_router': nrm(ks[15], (N_MOE, D_MODEL, N_EXPERTS), D_MODEL ** -0.5),
        'moe_w_gate': nrm(ks[16], (N_MOE, N_EXPERTS, D_MODEL, D_FF_EXPERT), D_MODEL ** -0.5),
        'moe_w_up': nrm(ks[17], (N_MOE, N_EXPERTS, D_MODEL, D_FF_EXPERT), D_MODEL ** -0.5),
        'moe_w_down': nrm(ks[18], (N_MOE, N_EXPERTS, D_FF_EXPERT, D_MODEL), D_FF_EXPERT ** -0.5),
        'final_norm_g': gain(ks[19], (D_MODEL,)),
    }


def reference(x, norm1_g, w_in, swa_sinks, gla_w_gate, gla_b_gate, gla_norm_g, w_branch_a, w_branch_b,
              w_branch_c, w_o, norm2_g, ffn_w_gate, ffn_w_up, ffn_w_down, moe_router, moe_w_gate, moe_w_up,
              moe_w_down, final_norm_g):
    for l in range(DEPTH):
        h = rms_norm(x, norm1_g[l])
        x = x + mixer_block(h, w_in[l], swa_sinks[l], gla_w_gate[l], gla_b_gate[l], gla_norm_g[l],
                            w_branch_a[l], w_branch_b[l], w_branch_c[l], w_o[l])
        h = rms_norm(x, norm2_g[l])
        j = l // 2
        if l % 2 == 0:
            x = x + swiglu(h, ffn_w_gate[j], ffn_w_up[j], ffn_w_down[j])
        else:
            x = x + moe_swiglu(h, moe_router[j], moe_w_gate[j], moe_w_up[j], moe_w_down[j])
    return rms_norm(x, final_norm_g)
```

```python
import functools

import jax
import jax.numpy as jnp
from jax import lax
from jax.experimental import pallas as pl
from jax.experimental.pallas import tpu as pltpu

F32 = jnp.float32
BF16 = jnp.bfloat16

D_MODEL = 1024
HEAD_DIM = 64
SWA_Q_HEADS = 8
SWA_KV_HEADS = 2
SWA_BLOCK = 128
MOBA_HEADS = 8
MOBA_BLOCK = 256
MOBA_TOPK = 3
GLA_HEADS = 4
GLA_DK = 64
GLA_DV = 128
GLA_GATE_RANK = 16
GLA_GATE_TEMP = 16.0
GLA_CHUNK = 64
N_EXPERTS = 8
D_FF = 2816
D_FF_EXPERT = 1408
RMS_EPS = 1e-6

LANES = 128
MASK_NEG = -1e30
VMEM_LIMIT = 48 * 1024 * 1024

COL_GATES = 0
COL_AQ = 3072
COL_AK = 3584
COL_AV = 3840
COL_BQ = 4096
COL_BK = 4608
COL_BV = 5120
COL_CQ = 5632
COL_CK = 5888
COL_CV = 6144
COL_CR = 6656
COL_CG = 7168
P_WIDTH = 7424
PROJ_TN = 256
N_GATE_TILES = 3 * D_MODEL // PROJ_TN


def _dot(a, b):
    return jnp.dot(a, b, preferred_element_type=F32)


def _dot_nt(a, b):
    return lax.dot_general(a, b, (((1,), (1,)), ((), ())), preferred_element_type=F32)


def _dot_tn(a, b):
    return lax.dot_general(a, b, (((0,), (0,)), ((), ())), preferred_element_type=F32)


def _rms(x, g):
    return x * lax.rsqrt(jnp.mean(x * x, axis=-1, keepdims=True) + RMS_EPS) * g


def _silu(x):
    return x * jax.nn.sigmoid(x)


def _params(*sem):
    return pltpu.CompilerParams(dimension_semantics=sem, vmem_limit_bytes=VMEM_LIMIT)


def _proj_kernel(x_ref, g_ref, w_ref, o_ref, h_ref):
    j = pl.program_id(1)

    @pl.when(j == 0)
    def _():
        h_ref[...] = _rms(x_ref[...], g_ref[...]).astype(BF16)

    acc = _dot(h_ref[...], w_ref[...])

    @pl.when(j < N_GATE_TILES)
    def _():
        o_ref[...] = jax.nn.sigmoid(acc).astype(o_ref.dtype)

    @pl.when(j >= N_GATE_TILES)
    def _():
        o_ref[...] = acc.astype(o_ref.dtype)


def _project(x2, g, w_all, tm):
    t = x2.shape[0]
    return pl.pallas_call(
        _proj_kernel,
        grid=(t // tm, P_WIDTH // PROJ_TN),
        in_specs=[
            pl.BlockSpec((tm, D_MODEL), lambda i, j: (i, 0)),
            pl.BlockSpec((1, D_MODEL), lambda i, j: (0, 0)),
            pl.BlockSpec((D_MODEL, PROJ_TN), lambda i, j: (0, j)),
        ],
        out_specs=pl.BlockSpec((tm, PROJ_TN), lambda i, j: (i, j)),
        out_shape=jax.ShapeDtypeStruct((t, P_WIDTH), BF16),
        scratch_shapes=[pltpu.VMEM((tm, D_MODEL), BF16)],
        compiler_params=_params("parallel", "arbitrary"),
        name="proj",
    )(x2, g, w_all)


def _swa_kernel(sink_ref, q_ref, kc_ref, kp_ref, vc_ref, vp_ref, o_ref):
    n = pl.program_id(1)
    blk = SWA_BLOCK
    lane = lax.broadcasted_iota(jnp.int32, (1, LANES), 1)
    first = lane < HEAD_DIM
    qi = lax.broadcasted_iota(jnp.int32, (blk, blk), 0)
    kj = lax.broadcasted_iota(jnp.int32, (blk, blk), 1)
    mask_cur = kj <= qi
    mask_prev = jnp.logical_and(kj > qi, n > 0)
    group = SWA_Q_HEADS // SWA_KV_HEADS
    for p in range(SWA_Q_HEADS // 2):
        g = (2 * p) // group
        qp = q_ref[:, p * LANES:(p + 1) * LANES]
        kc = kc_ref[:, g * LANES:(g + 1) * LANES]
        kp = kp_ref[:, g * LANES:(g + 1) * LANES]
        vc = vc_ref[:, g * LANES:(g + 1) * LANES]
        vp = vp_ref[:, g * LANES:(g + 1) * LANES]
        outs = []
        for hh in range(2):
            qm = jnp.where(first if hh == 0 else jnp.logical_not(first), qp, jnp.zeros_like(qp))
            sc = jnp.where(mask_cur, _dot_nt(qm, kc), MASK_NEG)
            sp = jnp.where(mask_prev, _dot_nt(qm, kp), MASK_NEG)
            sink = sink_ref[2 * p + hh]
            m = jnp.maximum(jnp.maximum(jnp.max(sc, axis=1, keepdims=True),
                                        jnp.max(sp, axis=1, keepdims=True)), sink)
            pc = jnp.exp(sc - m)
            pp = jnp.exp(sp - m)
            den = (jnp.sum(pc, axis=1, keepdims=True) + jnp.sum(pp, axis=1, keepdims=True)
                   + jnp.exp(sink - m))
            o = _dot(pc.astype(BF16), vc) + _dot(pp.astype(BF16), vp)
            outs.append(o * (1.0 / den))
        o_ref[:, p * LANES:(p + 1) * LANES] = jnp.where(first, outs[0], outs[1]).astype(o_ref.dtype)


def _swa(p3, sinks):
    b, s, _ = p3.shape
    blk = SWA_BLOCK
    qw = SWA_Q_HEADS * HEAD_DIM
    kw = 2 * SWA_KV_HEADS * HEAD_DIM
    prev = lambda bi, n: jnp.maximum(n - 1, 0)
    return pl.pallas_call(
        _swa_kernel,
        grid=(b, s // blk),
        in_specs=[
            pl.BlockSpec(memory_space=pltpu.SMEM),
            pl.BlockSpec((None, blk, qw), lambda bi, n: (bi, n, COL_AQ // qw)),
            pl.BlockSpec((None, blk, kw), lambda bi, n: (bi, n, COL_AK // kw)),
            pl.BlockSpec((None, blk, kw), lambda bi, n: (bi, prev(bi, n), COL_AK // kw)),
            pl.BlockSpec((None, blk, kw), lambda bi, n: (bi, n, COL_AV // kw)),
            pl.BlockSpec((None, blk, kw), lambda bi, n: (bi, prev(bi, n), COL_AV // kw)),
        ],
        out_specs=pl.BlockSpec((None, blk, qw), lambda bi, n: (bi, n, 0)),
        out_shape=jax.ShapeDtypeStruct((b, s, qw), BF16),
        compiler_params=_params("parallel", "arbitrary"),
        name="swa",
    )(sinks, p3, p3, p3, p3, p3)


def _moba_kernel(q_ref, k_ref, v_ref, o_ref, kbar_ref, *, nblk):
    cur = pl.program_id(2)
    blk = MOBA_BLOCK

    @pl.when(cur == 0)
    def _():
        kbar_ref[...] = jnp.zeros_like(kbar_ref)
        for j in range(nblk):
            kb = k_ref[j * blk:(j + 1) * blk, :].astype(F32)
            kbar_ref[j:j + 1, :] = jnp.mean(kb, axis=0, keepdims=True)

    lane = lax.broadcasted_iota(jnp.int32, (1, LANES), 1)
    first = lane < HEAD_DIM
    col = lax.broadcasted_iota(jnp.int32, (blk, LANES), 1)
    q = q_ref[...]
    kbar = kbar_ref[...].astype(BF16)
    qms, sels = [], []
    for hh in range(2):
        qm = jnp.where(first if hh == 0 else jnp.logical_not(first), q, jnp.zeros_like(q))
        qms.append(qm)
        gs = jnp.where(col < cur, _dot_nt(qm, kbar), -jnp.inf)
        sel = jnp.zeros((blk, LANES), F32)
        for _ in range(MOBA_TOPK):
            mx = jnp.max(gs, axis=1, keepdims=True)
            idx = jnp.min(jnp.where(gs == mx, col, LANES), axis=1, keepdims=True)
            pick = jnp.logical_and(col == idx, mx > -jnp.inf)
            sel = jnp.where(pick, 1.0, sel)
            gs = jnp.where(pick, -jnp.inf, gs)
        sels.append(sel)

    def attend(s, vj, carry):
        m, l, acc = carry
        m_new = jnp.maximum(m, jnp.max(s, axis=1, keepdims=True))
        alpha = jnp.exp(m - m_new)
        p = jnp.exp(s - m_new)
        l = alpha * l + jnp.sum(p, axis=1, keepdims=True)
        acc = alpha * acc + _dot(p.astype(BF16), vj)
        return m_new, l, acc

    def body(j, carry):
        start = pl.multiple_of(j * blk, blk)
        kj = k_ref[pl.ds(start, blk), :]
        vj = v_ref[pl.ds(start, blk), :]
        out = []
        for hh in range(2):
            chosen = jnp.sum(jnp.where(col == j, sels[hh], 0.0), axis=1, keepdims=True) > 0.0
            s = jnp.where(chosen, _dot_nt(qms[hh], kj), MASK_NEG)
            out.append(attend(s, vj, carry[hh]))
        return tuple(out)

    init = tuple((jnp.full((blk, 1), MASK_NEG, F32), jnp.zeros((blk, 1), F32),
                  jnp.zeros((blk, LANES), F32)) for _ in range(2))
    carry = lax.fori_loop(0, cur, body, init)

    start = pl.multiple_of(cur * blk, blk)
    kj = k_ref[pl.ds(start, blk), :]
    vj = v_ref[pl.ds(start, blk), :]
    qi = lax.broadcasted_iota(jnp.int32, (blk, blk), 0)
    ki = lax.broadcasted_iota(jnp.int32, (blk, blk), 1)
    outs = []
    for hh in range(2):
        s = jnp.where(ki <= qi, _dot_nt(qms[hh], kj), MASK_NEG)
        _, l, acc = attend(s, vj, carry[hh])
        outs.append(acc * (1.0 / l))
    o_ref[...] = jnp.where(first, outs[0], outs[1]).astype(o_ref.dtype)


def _moba(p3):
    b, s, _ = p3.shape
    blk = MOBA_BLOCK
    nblk = s // blk
    assert s % blk == 0 and nblk <= LANES
    npair = MOBA_HEADS // 2
    return pl.pallas_call(
        functools.partial(_moba_kernel, nblk=nblk),
        grid=(b, npair, nblk),
        in_specs=[
            pl.BlockSpec((None, blk, LANES), lambda bi, p, i: (bi, i, COL_BQ // LANES + p)),
            pl.BlockSpec((None, s, LANES), lambda bi, p, i: (bi, 0, COL_BK // LANES + p)),
            pl.BlockSpec((None, s, LANES), lambda bi, p, i: (bi, 0, COL_BV // LANES + p)),
        ],
        out_specs=pl.BlockSpec((None, blk, LANES), lambda bi, p, i: (bi, i, p)),
        out_shape=jax.ShapeDtypeStruct((b, s, MOBA_HEADS * HEAD_DIM), BF16),
        scratch_shapes=[pltpu.VMEM((LANES, LANES), F32)],
        compiler_params=_params("parallel", "parallel", "arbitrary"),
        name="moba",
    )(p3, p3, p3)


def _gla_kernel(q_ref, k_ref, v_ref, r_ref, cg_ref, wg_ref, bg_ref, ng_ref, o_ref, st_ref, gl_ref,
                *, nchunk):
    c = GLA_CHUNK

    @pl.when(pl.program_id(2) == 0)
    def _():
        st_ref[...] = jnp.zeros_like(st_ref)

    z = _dot(cg_ref[...], wg_ref[...]) + bg_ref[...]
    gl_ref[...] = jax.nn.log_sigmoid(z) * (1.0 / GLA_GATE_TEMP)

    lane = lax.broadcasted_iota(jnp.int32, (1, LANES), 1)
    first = lane < GLA_DK
    ti = lax.broadcasted_iota(jnp.int32, (c, c), 0)
    si = lax.broadcasted_iota(jnp.int32, (c, c), 1)
    causal = si <= ti
    tri = causal.astype(BF16)

    def chunk(ci, _):
        rows = pl.ds(pl.multiple_of(ci * c, c), c)
        g = gl_ref[rows, :]
        g_hi = g.astype(BF16)
        g_r = g - g_hi.astype(F32)
        g_mid = g_r.astype(BF16)
        g_lo = (g_r - g_mid.astype(F32)).astype(BF16)
        bcum = _dot(tri, g_hi) + _dot(tri, g_mid) + _dot(tri, g_lo)
        blast = bcum[c - 1:c, :]
        qc = q_ref[rows, :].astype(F32)
        kc = k_ref[rows, :].astype(F32)
        q_in = qc * jnp.exp(bcum)
        k_in = (kc * jnp.exp(-bcum)).astype(BF16)
        k_dec = (kc * jnp.exp(blast - bcum)).astype(BF16)
        decay = jnp.exp(blast)
        st = st_ref[...]
        st_b = st.astype(BF16)
        vc = v_ref[rows, :]
        rc = r_ref[rows, :].astype(F32)
        ut = []
        for hh in range(2):
            sl = slice(hh * GLA_DV, (hh + 1) * GLA_DV)
            qm = jnp.where(first if hh == 0 else jnp.logical_not(first), q_in, 0.0).astype(BF16)
            a = jnp.where(causal, _dot_nt(qm, k_in), 0.0)
            vh = vc[:, sl]
            o = _dot(a.astype(BF16), vh) + _dot_nt(qm, st_b)
            o = o * lax.rsqrt(jnp.mean(o * o, axis=-1, keepdims=True) + RMS_EPS)
            o = o * ng_ref[:, sl] * _silu(rc[:, sl])
            o_ref[rows, sl] = o.astype(o_ref.dtype)
            ut.append(_dot_tn(vh, k_dec))
        st_ref[...] = decay * st + jnp.where(first, ut[0], ut[1])
        return 0

    lax.fori_loop(0, nchunk, chunk, 0)


def _gla(p3, w_gate, b_gate, norm_g, ts):
    b, s, _ = p3.shape
    npair = GLA_HEADS // 2
    vw = 2 * GLA_DV
    return pl.pallas_call(
        functools.partial(_gla_kernel, nchunk=ts // GLA_CHUNK),
        grid=(b, npair, s // ts),
        in_specs=[
            pl.BlockSpec((None, ts, LANES), lambda bi, p, t: (bi, t, COL_CQ // LANES + p)),
            pl.BlockSpec((None, ts, LANES), lambda bi, p, t: (bi, t, COL_CK // LANES + p)),
            pl.BlockSpec((None, ts, vw), lambda bi, p, t: (bi, t, COL_CV // vw + p)),
            pl.BlockSpec((None, ts, vw), lambda bi, p, t: (bi, t, COL_CR // vw + p)),
            pl.BlockSpec((None, ts, LANES), lambda bi, p, t: (bi, t, COL_CG // LANES)),
            pl.BlockSpec((LANES, LANES), lambda bi, p, t: (0, p)),
            pl.BlockSpec((1, LANES), lambda bi, p, t: (0, p)),
            pl.BlockSpec((1, vw), lambda bi, p, t: (0, p)),
        ],
        out_specs=pl.BlockSpec((None, ts, vw), lambda bi, p, t: (bi, t, p)),
        out_shape=jax.ShapeDtypeStruct((b, s, GLA_HEADS * GLA_DV), BF16),
        scratch_shapes=[pltpu.VMEM((GLA_DV, LANES), F32), pltpu.VMEM((ts, LANES), F32)],
        compiler_params=_params("parallel", "parallel", "arbitrary"),
        name="gla",
    )(p3, p3, p3, p3, p3, w_gate, b_gate, norm_g)


def _router_combine(h, wr):
    h_hi = h.astype(BF16)
    h_lo = (h - h_hi.astype(F32)).astype(BF16)
    w_hi = wr.astype(BF16)
    w_lo = (wr - w_hi.astype(F32)).astype(BF16)
    logits = _dot(h_hi, w_hi) + _dot(h_lo, w_hi) + _dot(h_hi, w_lo)
    col = lax.broadcasted_iota(jnp.int32, logits.shape, 1)
    lg = jnp.where(col < N_EXPERTS, logits, -jnp.inf)
    m1 = jnp.max(lg, axis=1, keepdims=True)
    i1 = jnp.min(jnp.where(lg == m1, col, LANES), axis=1, keepdims=True)
    lg2 = jnp.where(col == i1, -jnp.inf, lg)
    m2 = jnp.max(lg2, axis=1, keepdims=True)
    i2 = jnp.min(jnp.where(lg2 == m2, col, LANES), axis=1, keepdims=True)
    e2 = jnp.exp(m2 - m1)
    p1 = 1.0 / (1.0 + e2)
    p2 = e2 * p1
    return jnp.where(col == i1, p1, 0.0) + jnp.where(col == i2, p2, 0.0)


def _merge_kernel(*refs, with_router):
    if with_router:
        (oa_ref, ob_ref, oc_ref, ga_ref, gb_ref, gc_ref, x_ref, wa_ref, wb_ref, wc_ref, wo_ref, g2_ref,
         wr_ref, x1_ref, h2_ref, cmb_ref) = refs
    else:
        (oa_ref, ob_ref, oc_ref, ga_ref, gb_ref, gc_ref, x_ref, wa_ref, wb_ref, wc_ref, wo_ref, g2_ref,
         x1_ref, h2_ref) = refs
    y = (ga_ref[...].astype(F32) * _dot(oa_ref[...], wa_ref[...])
         + gb_ref[...].astype(F32) * _dot(ob_ref[...], wb_ref[...])
         + gc_ref[...].astype(F32) * _dot(oc_ref[...], wc_ref[...]))
    x1 = x_ref[...] + _dot(y.astype(BF16), wo_ref[...])
    x1_ref[...] = x1
    h2 = _rms(x1, g2_ref[...])
    h2_ref[...] = h2.astype(BF16)
    if with_router:
        cmb_ref[...] = _router_combine(h2, wr_ref[...])


def _merge(oa, ob, oc, p2, x2, wa, wb, wc, wo, g2, w_router, tm):
    t = x2.shape[0]
    bw = oa.shape[1]
    with_router = w_router is not None
    row = lambda i: (i, 0)
    const = lambda i: (0, 0)
    in_specs = [
        pl.BlockSpec((tm, bw), row), pl.BlockSpec((tm, bw), row), pl.BlockSpec((tm, bw), row),
        pl.BlockSpec((tm, D_MODEL), lambda i: (i, 0)),
        pl.BlockSpec((tm, D_MODEL), lambda i: (i, 1)),
        pl.BlockSpec((tm, D_MODEL), lambda i: (i, 2)),
        pl.BlockSpec((tm, D_MODEL), row),
        pl.BlockSpec((bw, D_MODEL), const), pl.BlockSpec((bw, D_MODEL), const),
        pl.BlockSpec((bw, D_MODEL), const), pl.BlockSpec((D_MODEL, D_MODEL), const),
        pl.BlockSpec((1, D_MODEL), const),
    ]
    args = [oa, ob, oc, p2, p2, p2, x2, wa, wb, wc, wo, g2]
    out_specs = [pl.BlockSpec((tm, D_MODEL), row), pl.BlockSpec((tm, D_MODEL), row)]
    out_shape = [jax.ShapeDtypeStruct((t, D_MODEL), F32), jax.ShapeDtypeStruct((t, D_MODEL), BF16)]
    if with_router:
        in_specs.append(pl.BlockSpec((D_MODEL, LANES), const))
        args.append(w_router)
        out_specs.append(pl.BlockSpec((tm, LANES), row))
        out_shape.append(jax.ShapeDtypeStruct((t, LANES), F32))
    return pl.pallas_call(
        functools.partial(_merge_kernel, with_router=with_router),
        grid=(t // tm,),
        in_specs=in_specs,
        out_specs=out_specs,
        out_shape=out_shape,
        compiler_params=_params("parallel"),
        name="merge",
    )(*args)


def _ffn_kernel(h_ref, x_ref, wg_ref, wu_ref, wd_ref, fg_ref, o_ref, acc_ref, *, final_norm):
    j = pl.program_id(1)

    @pl.when(j == 0)
    def _():
        acc_ref[...] = jnp.zeros_like(acc_ref)

    h = h_ref[...]
    act = _silu(_dot(h, wg_ref[...])) * _dot(h, wu_ref[...])
    acc_ref[...] += _dot(act.astype(BF16), wd_ref[...])

    @pl.when(j == pl.num_programs(1) - 1)
    def _():
        y = x_ref[...] + acc_ref[...]
        o_ref[...] = _rms(y, fg_ref[...]) if final_norm else y


def _ffn(h2, x1, wg, wu, wd, fg, tm, tf, final_norm):
    t = h2.shape[0]
    f = wg.shape[1]
    return pl.pallas_call(
        functools.partial(_ffn_kernel, final_norm=final_norm),
        grid=(t // tm, f // tf),
        in_specs=[
            pl.BlockSpec((tm, D_MODEL), lambda i, j: (i, 0)),
            pl.BlockSpec((tm, D_MODEL), lambda i, j: (i, 0)),
            pl.BlockSpec((D_MODEL, tf), lambda i, j: (0, j)),
            pl.BlockSpec((D_MODEL, tf), lambda i, j: (0, j)),
            pl.BlockSpec((tf, D_MODEL), lambda i, j: (j, 0)),
            pl.BlockSpec((1, D_MODEL), lambda i, j: (0, 0)),
        ],
        out_specs=pl.BlockSpec((tm, D_MODEL), lambda i, j: (i, 0)),
        out_shape=jax.ShapeDtypeStruct((t, D_MODEL), F32),
        scratch_shapes=[pltpu.VMEM((tm, D_MODEL), F32)],
        compiler_params=_params("parallel", "arbitrary"),
        name="ffn",
    )(h2, x1, wg, wu, wd, fg)


def _moe_kernel(h_ref, x_ref, cmb_ref, wg_ref, wu_ref, wd_ref, fg_ref, o_ref, acc_ref, *, final_norm):
    e = pl.program_id(1)

    @pl.when(e == 0)
    def _():
        acc_ref[...] = jnp.zeros_like(acc_ref)

    h = h_ref[...]
    cmb = cmb_ref[...]
    col = lax.broadcasted_iota(jnp.int32, cmb.shape, 1)
    ce = jnp.sum(jnp.where(col == e, cmb, 0.0), axis=1, keepdims=True)
    act = _silu(_dot(h, wg_ref[...])) * _dot(h, wu_ref[...]) * ce
    acc_ref[...] += _dot(act.astype(BF16), wd_ref[...])

    @pl.when(e == pl.num_programs(1) - 1)
    def _():
        y = x_ref[...] + acc_ref[...]
        o_ref[...] = _rms(y, fg_ref[...]) if final_norm else y


def _moe(h2, x1, cmb, wg, wu, wd, fg, tm, final_norm):
    t = h2.shape[0]
    ne, _, fe = wg.shape
    return pl.pallas_call(
        functools.partial(_moe_kernel, final_norm=final_norm),
        grid=(t // tm, ne),
        in_specs=[
            pl.BlockSpec((tm, D_MODEL), lambda i, e: (i, 0)),
            pl.BlockSpec((tm, D_MODEL), lambda i, e: (i, 0)),
            pl.BlockSpec((tm, LANES), lambda i, e: (i, 0)),
            pl.BlockSpec((None, D_MODEL, fe), lambda i, e: (e, 0, 0)),
            pl.BlockSpec((None, D_MODEL, fe), lambda i, e: (e, 0, 0)),
            pl.BlockSpec((None, fe, D_MODEL), lambda i, e: (e, 0, 0)),
            pl.BlockSpec((1, D_MODEL), lambda i, e: (0, 0)),
        ],
        out_specs=pl.BlockSpec((tm, D_MODEL), lambda i, e: (i, 0)),
        out_shape=jax.ShapeDtypeStruct((t, D_MODEL), F32),
        scratch_shapes=[pltpu.VMEM((tm, D_MODEL), F32)],
        compiler_params=_params("parallel", "arbitrary"),
        name="moe",
    )(h2, x1, cmb, wg, wu, wd, fg)


def _pack_w_in(w_in):
    widths = (512, 128, 128, 512, 512, 512, 256, 256, 512, 512, GLA_GATE_RANK, 3 * D_MODEL)
    offs = [0]
    for w in widths:
        offs.append(offs[-1] + w)
    aq, ak, av, bq, bk, bv, cq, ck, cv, cr, cg, gates = (w_in[:, offs[i]:offs[i + 1]] for i in range(12))
    scale = HEAD_DIM ** -0.5
    dup = lambda kv: jnp.concatenate([kv[:, :64], kv[:, :64], kv[:, 64:], kv[:, 64:]], axis=1)
    cg_pad = jnp.pad(cg, ((0, 0), (0, 256 - GLA_GATE_RANK)))
    w_all = jnp.concatenate(
        [gates, aq * scale, dup(ak), dup(av), bq * scale, bk, bv, cq * (GLA_DK ** -0.5), ck, cv, cr, cg_pad],
        axis=1)
    assert w_all.shape[1] == P_WIDTH
    return w_all.astype(BF16)


def _layer(x2, b, s, lw, moe, final_g, tiles):
    t = x2.shape[0]
    p2 = _project(x2, lw["norm1_g"], lw["w_all"], tiles["proj_tm"])
    p3 = p2.reshape(b, s, P_WIDTH)
    oa = _swa(p3, lw["sinks"]).reshape(t, -1)
    ob = _moba(p3).reshape(t, -1)
    oc = _gla(p3, lw["w_gate"], lw["b_gate"], lw["gla_norm_g"], tiles["gla_ts"]).reshape(t, -1)
    final_norm = final_g is not None
    fg = final_g if final_norm else lw["norm2_g"]
    if moe:
        x1, h2, cmb = _merge(oa, ob, oc, p2, x2, lw["wa"], lw["wb"], lw["wc"], lw["wo"], lw["norm2_g"],
                             lw["w_router"], tiles["merge_tm"])
        return _moe(h2, x1, cmb, lw["wg"], lw["wu"], lw["wd"], fg, tiles["moe_tm"], final_norm)
    x1, h2 = _merge(oa, ob, oc, p2, x2, lw["wa"], lw["wb"], lw["wc"], lw["wo"], lw["norm2_g"], None,
                    tiles["merge_tm"])
    return _ffn(h2, x1, lw["wg"], lw["wu"], lw["wd"], fg, tiles["ffn_tm"], tiles["ffn_tf"], final_norm)


def _tiles(t, s):
    pick = lambda n, want: want if n % want == 0 else n
    return dict(proj_tm=pick(t, 1024), merge_tm=pick(t, 512), ffn_tm=pick(t, 512), ffn_tf=256,
                moe_tm=pick(t, 512), gla_ts=pick(s, 512))


def kernel(x, norm1_g, w_in, swa_sinks, gla_w_gate, gla_b_gate, gla_norm_g, w_branch_a, w_branch_b, w_branch_c,
           w_o, norm2_g, ffn_w_gate, ffn_w_up, ffn_w_down, moe_router, moe_w_gate, moe_w_up, moe_w_down,
           final_norm_g):
    b, s, d = x.shape
    depth = w_in.shape[0]
    t = b * s
    tiles = _tiles(t, s)
    x2 = x.reshape(t, d)
    for l in range(depth):
        moe = l % 2 == 1
        j = l // 2
        lw = dict(
            norm1_g=norm1_g[l].reshape(1, d),
            w_all=_pack_w_in(w_in[l]),
            sinks=swa_sinks[l],
            w_gate=jnp.pad(gla_w_gate[l], ((0, LANES - GLA_GATE_RANK), (0, 0))).astype(BF16),
            b_gate=gla_b_gate[l].reshape(1, -1),
            gla_norm_g=gla_norm_g[l].reshape(1, -1),
            wa=w_branch_a[l].astype(BF16), wb=w_branch_b[l].astype(BF16), wc=w_branch_c[l].astype(BF16),
            wo=w_o[l].astype(BF16),
            norm2_g=norm2_g[l].reshape(1, d),
        )
        if moe:
            lw.update(w_router=jnp.pad(moe_router[j], ((0, 0), (0, LANES - N_EXPERTS))),
                      wg=moe_w_gate[j].astype(BF16), wu=moe_w_up[j].astype(BF16), wd=moe_w_down[j].astype(BF16))
        else:
            lw.update(wg=ffn_w_gate[j].astype(BF16), wu=ffn_w_up[j].astype(BF16), wd=ffn_w_down[j].astype(BF16))
        final_g = final_norm_g.reshape(1, d) if l == depth - 1 else None
        x2 = _layer(x2, b, s, lw, moe, final_g, tiles)
    return x2.reshape(b, s, d)
```
